```python
import jax, jax.numpy as jnp
from jax import lax
import numpy as np

D_MODEL = 1024
BATCH = 4
SEQ = 8192
DEPTH = 1

CHUNK = 64
Q_BLOCK = 128
PLE_DIM = 256
CONV_DIM = D_MODEL // 2
CONV_WIDTH = 3
SB_HEADS = 8
SB_HEAD_DIM = 64
SB_DIM = SB_HEADS * SB_HEAD_DIM
MIX_WIDTH = CONV_DIM + SB_DIM
IN_PROJ_DIM = 3 * CONV_DIM + 3 * SB_DIM
D_FF = 2816
FFN_RES = 0.5
EPS = 1e-6

kernel_name = "hybrid_shortconv_stickbreaking_macaron_block"


def _rmsnorm(x, g):
    xf = x.astype(jnp.float32)
    y = xf * lax.rsqrt(jnp.mean(xf * xf, axis=-1, keepdims=True) + EPS)
    return (y * g.astype(jnp.float32)).astype(x.dtype)


def _swiglu(h, w_gate, w_up, w_down):
    return (jax.nn.silu(h @ w_gate) * (h @ w_up)) @ w_down


def _short_gated_conv(b, c, u, conv_w, conv_b):
    z = c * u
    rhs = conv_w[:, None, :].astype(z.dtype)
    y = lax.conv_general_dilated(
        z, rhs, window_strides=(1,), padding=[(CONV_WIDTH - 1, 0)],
        dimension_numbers=("NWC", "WIO", "NWC"), feature_group_count=CONV_DIM)
    return b * (y + conv_b.astype(z.dtype))


def _stick_breaking(q, k, v):
    S = q.shape[2]
    scale = SB_HEAD_DIM ** -0.5
    outs = []
    for t0 in range(0, S, Q_BLOCK):
        kl = t0 + Q_BLOCK
        qs = q[:, :, t0:t0 + Q_BLOCK]
        ks = k[:, :, :kl]
        vs = v[:, :, :kl]
        z = jnp.einsum("bhqd,bhkd->bhqk", qs, ks) * scale
        t_idx = t0 + jnp.arange(Q_BLOCK)[:, None]
        s_idx = jnp.arange(kl)[None, :]
        causal = s_idx < t_idx
        log_keep = jnp.where(causal, jax.nn.log_sigmoid(-z), 0.0)
        later = lax.cumsum(log_keep, axis=3, reverse=True) - log_keep
        a = jnp.where(causal, jnp.exp(jax.nn.log_sigmoid(z) + later), 0.0)
        outs.append(jnp.einsum("bhqk,bhkd->bhqd", a, vs))
    return jnp.concatenate(outs, axis=2)


def setup_inputs(seed: int = 0) -> dict:
    key = jax.random.key(seed)
    ks = jax.random.split(key, 24)
    f32 = jnp.float32

    def w(k, shape, fan_in):
        return jax.random.normal(k, shape, f32) * (fan_in ** -0.5)

    def gain(k, shape):
        return 1.0 + 0.05 * jax.random.normal(k, shape, f32)

    return {
        "x": jax.random.normal(ks[0], (BATCH, SEQ, D_MODEL), f32),
        "p": jax.random.normal(ks[1], (DEPTH, BATCH, SEQ, PLE_DIM), f32),
        "ffn1_norm": gain(ks[2], (DEPTH, D_MODEL)),
        "ffn1_w_gate": w(ks[3], (DEPTH, D_MODEL, D_FF), D_MODEL),
        "ffn1_w_up": w(ks[4], (DEPTH, D_MODEL, D_FF), D_MODEL),
        "ffn1_w_down": w(ks[5], (DEPTH, D_FF, D_MODEL), D_FF),
        "mix_norm": gain(ks[6], (DEPTH, D_MODEL)),
        "w_in": w(ks[7], (DEPTH, D_MODEL, IN_PROJ_DIM), D_MODEL),
        "conv_w": w(ks[8], (DEPTH, CONV_WIDTH, CONV_DIM), CONV_WIDTH),
        "conv_b": 0.02 * jax.random.normal(ks[9], (DEPTH, CONV_DIM), f32),
        "q_norm": gain(ks[10], (DEPTH, SB_HEAD_DIM)),
        "k_norm": gain(ks[11], (DEPTH, SB_HEAD_DIM)),
        "w_out": w(ks[12], (DEPTH, MIX_WIDTH, D_MODEL), MIX_WIDTH),
        "ffn2_norm": gain(ks[13], (DEPTH, D_MODEL)),
        "ffn2_w_gate": w(ks[14], (DEPTH, D_MODEL, D_FF), D_MODEL),
        "ffn2_w_up": w(ks[15], (DEPTH, D_MODEL, D_FF), D_MODEL),
        "ffn2_w_down": w(ks[16], (DEPTH, D_FF, D_MODEL), D_FF),
        "ple_norm": gain(ks[17], (DEPTH, D_MODEL)),
        "ple_w_gate": w(ks[18], (DEPTH, D_MODEL, D_MODEL), D_MODEL),
        "ple_w_proj": w(ks[19], (DEPTH, PLE_DIM, D_MODEL), PLE_DIM),
    }


def reference(x, p, ffn1_norm, ffn1_w_gate, ffn1_w_up, ffn1_w_down, mix_norm, w_in,
              conv_w, conv_b, q_norm, k_norm, w_out, ffn2_norm, ffn2_w_gate, ffn2_w_up,
              ffn2_w_down, ple_norm, ple_w_gate, ple_w_proj):
    B, S, _ = x.shape
    for i in range(DEPTH):
        x = x + FFN_RES * _swiglu(_rmsnorm(x, ffn1_norm[i]), ffn1_w_gate[i], ffn1_w_up[i], ffn1_w_down[i])

        h = _rmsnorm(x, mix_norm[i])
        proj = h @ w_in[i]
        b_g, c_g, u, q, k, v = jnp.split(
            proj, np.cumsum([CONV_DIM, CONV_DIM, CONV_DIM, SB_DIM, SB_DIM]).tolist(), axis=-1)

        y_conv = _short_gated_conv(b_g, c_g, u, conv_w[i], conv_b[i])

        def heads(t):
            return t.reshape(B, S, SB_HEADS, SB_HEAD_DIM).transpose(0, 2, 1, 3).astype(jnp.float32)
        qh = _rmsnorm(heads(q), q_norm[i])
        kh = _rmsnorm(heads(k), k_norm[i])
        y_sb = _stick_breaking(qh, kh, heads(v))
        y_sb = y_sb.transpose(0, 2, 1, 3).reshape(B, S, SB_DIM).astype(x.dtype)

        x = x + jnp.concatenate([y_conv, y_sb], axis=-1) @ w_out[i]

        x = x + FFN_RES * _swiglu(_rmsnorm(x, ffn2_norm[i]), ffn2_w_gate[i], ffn2_w_up[i], ffn2_w_down[i])

        gate = jax.nn.sigmoid(_rmsnorm(x, ple_norm[i]) @ ple_w_gate[i])
        x = x + gate * (p[i].astype(x.dtype) @ ple_w_proj[i])
    return x
```

```python
import functools
import math

import jax
import jax.numpy as jnp
from jax import lax
from jax.experimental import pallas as pl
from jax.experimental.pallas import tpu as pltpu

F32 = jnp.float32
BF16 = jnp.bfloat16

EPS = 1e-6
FFN_RES = 0.5
HEAD_DIM = 64
CONV_WIDTH = 3
LANES = 128
HEADS_PER_STEP = LANES // HEAD_DIM
ROW_TILE = 512
FF_CHUNK = 256
ATTN_BLOCK = 256
VMEM_LIMIT_BYTES = 56 * 1024 * 1024


def _rmsnorm(x, g):
    ms = jnp.mean(x * x, axis=-1, keepdims=True)
    return x * lax.rsqrt(ms + EPS) * g


def _split_bf16(x):
    hi = x.astype(BF16)
    lo = (x - hi.astype(F32)).astype(BF16)
    return hi, lo


def _dot(a, b):
    return jnp.dot(a, b, preferred_element_type=F32)


def _swiglu_half_step(x, g_ref, wg_ref, wu_ref, wd_ref, act_ref):
    h = _rmsnorm(x, g_ref[...]).astype(BF16)
    d_ff = wg_ref.shape[1]
    for c in range(d_ff // FF_CHUNK):
        sl = slice(c * FF_CHUNK, (c + 1) * FF_CHUNK)
        gate = _dot(h, wg_ref[:, sl])
        up = _dot(h, wu_ref[:, sl])
        act_ref[:, sl] = (gate * jax.nn.sigmoid(gate) * up).astype(BF16)
    return x + FFN_RES * _dot(act_ref[...], wd_ref[...])


def _head_rmsnorm(t, gain, bd_ref):
    hi, lo = _split_bf16(t * t)
    ss = _dot(hi, bd_ref[...]) + _dot(lo, bd_ref[...])
    return t * lax.rsqrt(ss * (1.0 / HEAD_DIM) + EPS) * gain


def _pre_kernel(x_ref, g1_ref, wg_ref, wu_ref, wd_ref, gmix_ref, win_ref, cw_ref, cb_ref,
                qg_ref, kg_ref, bd_ref,
                x1_ref, yc_ref, q_ref, k_ref, v_ref,
                act_ref, tail_ref):
    s = pl.program_id(1)
    x1 = _swiglu_half_step(x_ref[0], g1_ref, wg_ref, wu_ref, wd_ref, act_ref)
    x1_ref[0] = x1

    h = _rmsnorm(x1, gmix_ref[...]).astype(BF16)
    rows = h.shape[0]
    cdim = cw_ref.shape[1]

    def proj(i):
        return _dot(h, win_ref[:, i * cdim:(i + 1) * cdim])

    @pl.when(s == 0)
    def _():
        tail_ref[...] = jnp.zeros_like(tail_ref)

    z = proj(1) * proj(2)
    prev1 = tail_ref[7:8, :]
    prev2 = tail_ref[6:7, :]
    row = lax.broadcasted_iota(jnp.int32, z.shape, 0)
    z1 = jnp.where(row == 0, prev1, pltpu.roll(z, 1, 0))
    z2 = jnp.where(row == 0, prev2, jnp.where(row == 1, prev1, pltpu.roll(z, 2, 0)))
    tail_ref[...] = z[rows - 8:, :]
    conv = cw_ref[2:3, :] * z + cw_ref[1:2, :] * z1 + cw_ref[0:1, :] * z2 + cb_ref[...]
    yc_ref[0] = (proj(0) * conv).astype(BF16)

    q_ref[0] = _head_rmsnorm(proj(3), qg_ref[...], bd_ref).astype(BF16)
    k_ref[0] = _head_rmsnorm(proj(4), kg_ref[...], bd_ref).astype(BF16)
    v_ref[0] = proj(5).astype(BF16)


def _attn_kernel(q_ref, k_ref, v_ref, tri_ref, o_ref):
    i = pl.program_id(2)
    blk = q_ref.shape[1]
    q = q_ref[0]
    lane = lax.broadcasted_iota(jnp.int32, q.shape, 1)
    zero = jnp.zeros_like(q)
    q_heads = [jnp.where((lane // HEAD_DIM) == h, q, zero) for h in range(HEADS_PER_STEP)]
    vlane = lax.broadcasted_iota(jnp.int32, (blk, LANES), 1)
    tri = tri_ref[...]

    def sweep_block(j, carries, acc, causal):
        start = pl.multiple_of(j * blk, blk)
        kb = k_ref[0, pl.ds(start, blk), :]
        vb = v_ref[0, pl.ds(start, blk), :]
        vzero = jnp.zeros_like(vb)
        probs, vparts, new_carries = [], [], []
        for h in range(HEADS_PER_STEP):
            z = lax.dot_general(q_heads[h], kb, (((1,), (1,)), ((), ())), preferred_element_type=F32)
            softplus = jnp.maximum(z, 0.0) + jnp.log(1.0 + jnp.exp(-jnp.abs(z)))
            log_keep = -softplus
            if causal is not None:
                log_keep = jnp.where(causal, log_keep, 0.0)
            hi, lo = _split_bf16(log_keep)
            later = carries[h] + _dot(hi, tri) + _dot(lo, tri)
            a = jnp.exp(z - softplus + later)
            if causal is not None:
                a = jnp.where(causal, a, 0.0)
            probs.append(a.astype(BF16))
            vparts.append(jnp.where((vlane // HEAD_DIM) == h, vb, vzero))
            new_carries.append(carries[h] + jnp.sum(log_keep, axis=-1, keepdims=True))
        acc = acc + _dot(jnp.concatenate(probs, axis=1), jnp.concatenate(vparts, axis=0))
        return tuple(new_carries), acc

    r = lax.broadcasted_iota(jnp.int32, (blk, blk), 0)
    c = lax.broadcasted_iota(jnp.int32, (blk, blk), 1)
    carries = tuple(jnp.zeros((blk, 1), F32) for _ in range(HEADS_PER_STEP))
    carries, acc = sweep_block(i, carries, jnp.zeros((blk, LANES), F32), c < r)

    def body(t, state):
        return sweep_block(i - 1 - t, state[0], state[1], None)

    carries, acc = lax.fori_loop(0, i, body, (carries, acc))
    o_ref[0] = acc.astype(BF16)


def _post_kernel(x1_ref, yc_ref, ysb_ref, p_ref, wo_ref, g2_ref, wg_ref, wu_ref, wd_ref,
                 gp_ref, wpg_ref, wpp_ref, o_ref, act_ref):
    cdim = yc_ref.shape[2]
    x2 = x1_ref[0] + _dot(yc_ref[0], wo_ref[:cdim, :]) + _dot(ysb_ref[0], wo_ref[cdim:, :])
    x3 = _swiglu_half_step(x2, g2_ref, wg_ref, wu_ref, wd_ref, act_ref)
    gate = jax.nn.sigmoid(_dot(_rmsnorm(x3, gp_ref[...]).astype(BF16), wpg_ref[...]))
    o_ref[0] = x3 + gate * _dot(p_ref[0].astype(BF16), wpp_ref[...])


def _resident(shape):
    return pl.BlockSpec(shape, lambda *_: (0,) * len(shape), pipeline_mode=pl.Buffered(1))


def _row_spec(width):
    return pl.BlockSpec((1, ROW_TILE, width), lambda b, s: (b, s, 0))


def _layer(x, p, w):
    B, S, D = x.shape
    d_ff = w["ffn1_wg"].shape[1]
    cdim = w["conv_w"].shape[1]
    sdim = w["bd"].shape[0]
    assert S % ROW_TILE == 0 and S % ATTN_BLOCK == 0 and d_ff % FF_CHUNK == 0
    assert sdim % LANES == 0 and cdim == sdim
    grid = (B, S // ROW_TILE)
    dense_params = pltpu.CompilerParams(
        dimension_semantics=("arbitrary", "arbitrary"), vmem_limit_bytes=VMEM_LIMIT_BYTES)

    pre_weights = [w["ffn1_norm"], w["ffn1_wg"], w["ffn1_wu"], w["ffn1_wd"], w["mix_norm"], w["w_in"],
                   w["conv_w"], w["conv_b"], w["q_gain"], w["k_gain"], w["bd"]]
    x1, yc, q, k, v = pl.pallas_call(
        _pre_kernel,
        grid=grid,
        in_specs=[_row_spec(D)] + [_resident(a.shape) for a in pre_weights],
        out_specs=[_row_spec(D), _row_spec(cdim), _row_spec(sdim), _row_spec(sdim), _row_spec(sdim)],
        out_shape=[jax.ShapeDtypeStruct((B, S, D), F32),
                   jax.ShapeDtypeStruct((B, S, cdim), BF16),
                   jax.ShapeDtypeStruct((B, S, sdim), BF16),
                   jax.ShapeDtypeStruct((B, S, sdim), BF16),
                   jax.ShapeDtypeStruct((B, S, sdim), BF16)],
        scratch_shapes=[pltpu.VMEM((ROW_TILE, d_ff), BF16), pltpu.VMEM((8, cdim), F32)],
        compiler_params=dense_params,
        name="pre",
    )(x, *pre_weights)

    blk = ATTN_BLOCK
    ysb = pl.pallas_call(
        _attn_kernel,
        grid=(B, sdim // LANES, S // blk),
        in_specs=[pl.BlockSpec((1, blk, LANES), lambda b, hp, i: (b, i, hp)),
                  pl.BlockSpec((1, S, LANES), lambda b, hp, i: (b, 0, hp)),
                  pl.BlockSpec((1, S, LANES), lambda b, hp, i: (b, 0, hp)),
                  pl.BlockSpec((blk, blk), lambda b, hp, i: (0, 0))],
        out_specs=pl.BlockSpec((1, blk, LANES), lambda b, hp, i: (b, i, hp)),
        out_shape=jax.ShapeDtypeStruct((B, S, sdim), BF16),
        compiler_params=pltpu.CompilerParams(
            dimension_semantics=("arbitrary", "arbitrary", "arbitrary"),
            vmem_limit_bytes=VMEM_LIMIT_BYTES),
        name="attn",
    )(q, k, v, w["tri"])

    post_weights = [w["w_out"], w["ffn2_norm"], w["ffn2_wg"], w["ffn2_wu"], w["ffn2_wd"],
                    w["ple_norm"], w["ple_wg"], w["ple_wp"]]
    return pl.pallas_call(
        _post_kernel,
        grid=grid,
        in_specs=[_row_spec(D), _row_spec(cdim), _row_spec(sdim), _row_spec(p.shape[-1])]
        + [_resident(a.shape) for a in post_weights],
        out_specs=_row_spec(D),
        out_shape=jax.ShapeDtypeStruct((B, S, D), F32),
        scratch_shapes=[pltpu.VMEM((ROW_TILE, d_ff), BF16)],
        compiler_params=dense_params,
        name="post",
    )(x1, yc, ysb, p, *post_weights)


def kernel(x, p, ffn1_norm, ffn1_w_gate, ffn1_w_up, ffn1_w_down, mix_norm, w_in, conv_w, conv_b, q_norm, k_norm, w_out, ffn2_norm, ffn2_w_gate, ffn2_w_up, ffn2_w_down, ple_norm, ple_w_gate, ple_w_proj):
    depth = p.shape[0]
    sdim = w_out.shape[1] - conv_w.shape[2]
    heads = sdim // HEAD_DIM
    head_of = jnp.arange(sdim, dtype=jnp.int32) // HEAD_DIM
    bd = (head_of[:, None] == head_of[None, :]).astype(BF16)
    pos = jnp.arange(ATTN_BLOCK, dtype=jnp.int32)
    tri = (pos[:, None] > pos[None, :]).astype(BF16)
    row = lambda a: a.reshape(1, -1).astype(F32)
    for i in range(depth):
        w = dict(
            ffn1_norm=row(ffn1_norm[i]), ffn1_wg=ffn1_w_gate[i].astype(BF16),
            ffn1_wu=ffn1_w_up[i].astype(BF16), ffn1_wd=ffn1_w_down[i].astype(BF16),
            mix_norm=row(mix_norm[i]), w_in=w_in[i].astype(BF16),
            conv_w=conv_w[i].astype(F32), conv_b=row(conv_b[i]),
            q_gain=row(jnp.tile(q_norm[i], heads)) * (HEAD_DIM ** -0.5),
            k_gain=row(jnp.tile(k_norm[i], heads)),
            bd=bd, tri=tri,
            w_out=w_out[i].astype(BF16), ffn2_norm=row(ffn2_norm[i]),
            ffn2_wg=ffn2_w_gate[i].astype(BF16), ffn2_wu=ffn2_w_up[i].astype(BF16),
            ffn2_wd=ffn2_w_down[i].astype(BF16), ple_norm=row(ple_norm[i]),
            ple_wg=ple_w_gate[i].astype(BF16), ple_wp=ple_w_proj[i].astype(BF16))
        x = _layer(x, p[i], w)
    return x
```

```python
import math

import jax
import jax.numpy as jnp
from jax import lax
from jax.experimental import pallas as pl
from jax.experimental.pallas import tpu as pltpu

F32 = jnp.float32
BF16 = jnp.bfloat16

EPS = 1e-6
FFN_RES = 0.5
HEAD_DIM = 64
LANES = 128
HEADS_PER_STEP = LANES // HEAD_DIM
ROW_TILE = 512
FF_CHUNK = 256
ATTN_BLOCK = 256
VMEM_LIMIT_BYTES = 56 * 1024 * 1024
MASK_BIG = 1e30
LOG2_E = math.log2(math.e)
LN_2 = math.log(2.0)


def _rmsnorm(x, g):
    ms = jnp.mean(x * x, axis=-1, keepdims=True)
    return x * lax.rsqrt(ms + EPS) * g


def _split_bf16(x):
    hi = x.astype(BF16)
    lo = (x - hi.astype(F32)).astype(BF16)
    return hi, lo


def _dot(a, b):
    return jnp.dot(a, b, preferred_element_type=F32)


def _dot_nt(a, b):
    return lax.dot_general(a, b, (((1,), (1,)), ((), ())), preferred_element_type=F32)


def _swiglu_half_step(x, g_ref, wg_ref, wu_ref, wd_ref, act_ref):
    h = _rmsnorm(x, g_ref[...]).astype(BF16)
    d_ff = wg_ref.shape[1]
    for c in range(d_ff // FF_CHUNK):
        sl = slice(c * FF_CHUNK, (c + 1) * FF_CHUNK)
        gate = _dot(h, wg_ref[:, sl])
        up = _dot(h, wu_ref[:, sl])
        act_ref[:, sl] = (gate * jax.nn.sigmoid(gate) * up).astype(BF16)
    return x + FFN_RES * _dot(act_ref[...], wd_ref[...])


def _head_rmsnorm(t, gain, bd_ref):
    hi, lo = _split_bf16(t * t)
    ss = _dot(hi, bd_ref[...]) + _dot(lo, bd_ref[...])
    return t * lax.rsqrt(ss * (1.0 / HEAD_DIM) + EPS) * gain


def _store_per_head(dst_ref, t):
    lane = lax.broadcasted_iota(jnp.int32, (t.shape[0], LANES), 1)
    for h in range(t.shape[1] // HEAD_DIM):
        g = h // HEADS_PER_STEP
        group = t[:, g * LANES:(g + 1) * LANES]
        mine = (lane // HEAD_DIM) == (h % HEADS_PER_STEP)
        dst_ref[0, h] = jnp.where(mine, group, 0.0).astype(BF16)


def _pre_kernel(x_ref, g1_ref, wg_ref, wu_ref, wd_ref, gmix_ref, win_ref, cw_ref, cb_ref,
                qg_ref, kg_ref, bd_ref,
                x1_ref, yc_ref, q_ref, k_ref, v_ref,
                act_ref, tail_ref):
    s = pl.program_id(1)
    x1 = _swiglu_half_step(x_ref[0], g1_ref, wg_ref, wu_ref, wd_ref, act_ref)
    x1_ref[0] = x1

    h = _rmsnorm(x1, gmix_ref[...]).astype(BF16)
    rows = h.shape[0]
    cdim = cw_ref.shape[1]

    def proj(i):
        return _dot(h, win_ref[:, i * cdim:(i + 1) * cdim])

    @pl.when(s == 0)
    def _():
        tail_ref[...] = jnp.zeros_like(tail_ref)

    z = proj(1) * proj(2)
    prev1 = tail_ref[7:8, :]
    prev2 = tail_ref[6:7, :]
    row = lax.broadcasted_iota(jnp.int32, z.shape, 0)
    z1 = jnp.where(row == 0, prev1, pltpu.roll(z, 1, 0))
    z2 = jnp.where(row == 0, prev2, jnp.where(row == 1, prev1, pltpu.roll(z, 2, 0)))
    tail_ref[...] = z[rows - 8:, :]
    conv = cw_ref[2:3, :] * z + cw_ref[1:2, :] * z1 + cw_ref[0:1, :] * z2 + cb_ref[...]
    yc_ref[0] = (proj(0) * conv).astype(BF16)

    _store_per_head(q_ref, _head_rmsnorm(proj(3), qg_ref[...], bd_ref))
    k_ref[0] = _head_rmsnorm(proj(4), kg_ref[...], bd_ref).astype(BF16)
    _store_per_head(v_ref, proj(5))


def _attn_kernel(q_ref, k_ref, v_ref, tri_ref, bias_ref, o_ref, z_ref, spb_ref, base_ref, a_ref):
    blk = a_ref.shape[1]
    nq = k_ref.shape[1] // blk
    npairs = nq * (nq + 1) // 2

    z_ref[...] = jnp.zeros_like(z_ref)
    spb_ref[...] = jnp.zeros_like(spb_ref)
    base_ref[...] = jnp.zeros_like(base_ref)
    a_ref[...] = jnp.zeros_like(a_ref)

    def rows_of(ref, idx):
        start = pl.multiple_of(idx * blk, blk)
        return jnp.concatenate([ref[0, h, pl.ds(start, blk), :] for h in range(HEADS_PER_STEP)], axis=0)

    def step(state, wr):
        i0, j0, i1, j1, i2, j2, i3, j3, carry, acc = state
        rd = 1 - wr

        pv = _dot(a_ref[rd], rows_of(v_ref, j3))
        acc = jnp.where(j3 == i3, pv, acc + pv)
        o_ref[0, pl.ds(pl.multiple_of(i3 * blk, blk), blk), :] = acc.astype(BF16)

        cum = _dot(spb_ref[rd], tri_ref[...])
        prob = jnp.exp2(base_ref[rd] - cum).astype(BF16)
        a_ref[wr] = jnp.concatenate([prob[h * blk:(h + 1) * blk] for h in range(HEADS_PER_STEP)], axis=1)

        diag = j1 == i1
        z = jnp.minimum(z_ref[rd], bias_ref[diag.astype(jnp.int32)])
        sp = jnp.maximum(z, 0.0) + jnp.log(1.0 + jnp.exp2(-jnp.abs(z))) * (1.0 / LN_2)
        carry = jnp.where(diag, 0.0, carry)
        base_ref[wr] = (z - sp) - carry
        spb_ref[wr] = sp.astype(BF16)
        carry = carry + jnp.sum(sp, axis=-1, keepdims=True)

        iq = jnp.minimum(i0, nq - 1)
        jk = jnp.minimum(j0, nq - 1)
        kb = k_ref[0, pl.ds(pl.multiple_of(jk * blk, blk), blk), :]
        z_ref[wr] = _dot_nt(rows_of(q_ref, iq), kb)

        last = j0 == 0
        return (jnp.where(last, i0 + 1, i0), jnp.where(last, i0 + 1, j0 - 1),
                i0, j0, i1, j1, i2, j2, carry, acc)

    zero = jnp.int32(0)
    init = (zero,) * 8 + (jnp.zeros((HEADS_PER_STEP * blk, 1), F32), jnp.zeros((blk, LANES), F32))
    nsteps = npairs + 3
    state = lax.fori_loop(0, nsteps // 2, lambda _, st: step(step(st, 0), 1), init)
    if nsteps % 2:
        step(state, 0)


def _post_kernel(x1_ref, yc_ref, ysb_ref, p_ref, wo_ref, g2_ref, wg_ref, wu_ref, wd_ref,
                 gp_ref, wpg_ref, wpp_ref, o_ref, act_ref):
    cdim = yc_ref.shape[2]
    x2 = x1_ref[0] + _dot(yc_ref[0], wo_ref[:cdim, :]) + _dot(ysb_ref[0], wo_ref[cdim:, :])
    x3 = _swiglu_half_step(x2, g2_ref, wg_ref, wu_ref, wd_ref, act_ref)
    gate = jax.nn.sigmoid(_dot(_rmsnorm(x3, gp_ref[...]).astype(BF16), wpg_ref[...]))
    o_ref[0] = x3 + gate * _dot(p_ref[0].astype(BF16), wpp_ref[...])


def _resident(shape):
    return pl.BlockSpec(shape, lambda *_: (0,) * len(shape), pipeline_mode=pl.Buffered(1))


def _row_spec(width):
    return pl.BlockSpec((1, ROW_TILE, width), lambda b, s: (b, s, 0))


def _attention_constants():
    pos = jnp.arange(ATTN_BLOCK, dtype=jnp.int32)
    tri = (pos[:, None] > pos[None, :]).astype(BF16)
    causal = jnp.tile(pos[None, :] < pos[:, None], (HEADS_PER_STEP, 1))
    mask_bias = jnp.stack([jnp.full(causal.shape, MASK_BIG, F32),
                           jnp.where(causal, MASK_BIG, -MASK_BIG).astype(F32)])
    return tri, mask_bias


def _attention(q, k, v, tri, mask_bias):
    B, heads, S, _ = q.shape
    blk = ATTN_BLOCK
    rows = HEADS_PER_STEP * blk
    pair_spec = pl.BlockSpec((1, HEADS_PER_STEP, S, LANES), lambda b, hp: (b, hp, 0, 0))
    seq_spec = pl.BlockSpec((1, S, LANES), lambda b, hp: (b, 0, hp))
    return pl.pallas_call(
        _attn_kernel,
        grid=(B, heads // HEADS_PER_STEP),
        in_specs=[pair_spec, seq_spec, pair_spec, _resident(tri.shape), _resident(mask_bias.shape)],
        out_specs=seq_spec,
        out_shape=jax.ShapeDtypeStruct(k.shape, BF16),
        scratch_shapes=[pltpu.VMEM((2, rows, blk), F32), pltpu.VMEM((2, rows, blk), BF16),
                        pltpu.VMEM((2, rows, blk), F32), pltpu.VMEM((2, blk, rows), BF16)],
        compiler_params=pltpu.CompilerParams(
            dimension_semantics=("arbitrary", "arbitrary"), vmem_limit_bytes=VMEM_LIMIT_BYTES),
        name="attn",
    )(q, k, v, tri, mask_bias)


def _layer(x, p, w):
    B, S, D = x.shape
    d_ff = w["ffn1_wg"].shape[1]
    cdim = w["conv_w"].shape[1]
    sdim = w["bd"].shape[0]
    heads = sdim // HEAD_DIM
    assert S % ROW_TILE == 0 and S % ATTN_BLOCK == 0 and d_ff % FF_CHUNK == 0
    assert sdim % LANES == 0 and cdim == sdim
    grid = (B, S // ROW_TILE)
    dense_params = pltpu.CompilerParams(
        dimension_semantics=("arbitrary", "arbitrary"), vmem_limit_bytes=VMEM_LIMIT_BYTES)
    per_head_spec = pl.BlockSpec((1, heads, ROW_TILE, LANES), lambda b, s: (b, 0, s, 0))
    per_head_shape = jax.ShapeDtypeStruct((B, heads, S, LANES), BF16)

    pre_weights = [w["ffn1_norm"], w["ffn1_wg"], w["ffn1_wu"], w["ffn1_wd"], w["mix_norm"], w["w_in"],
                   w["conv_w"], w["conv_b"], w["q_gain"], w["k_gain"], w["bd"]]
    x1, yc, q, k, v = pl.pallas_call(
        _pre_kernel,
        grid=grid,
        in_specs=[_row_spec(D)] + [_resident(a.shape) for a in pre_weights],
        out_specs=[_row_spec(D), _row_spec(cdim), per_head_spec, _row_spec(sdim), per_head_spec],
        out_shape=[jax.ShapeDtypeStruct((B, S, D), F32),
                   jax.ShapeDtypeStruct((B, S, cdim), BF16),
                   per_head_shape,
                   jax.ShapeDtypeStruct((B, S, sdim), BF16),
                   per_head_shape],
        scratch_shapes=[pltpu.VMEM((ROW_TILE, d_ff), BF16), pltpu.VMEM((8, cdim), F32)],
        compiler_params=dense_params,
        name="pre",
    )(x, *pre_weights)

    ysb = _attention(q, k, v, w["tri"], w["mask_bias"])

    post_weights = [w["w_out"], w["ffn2_norm"], w["ffn2_wg"], w["ffn2_wu"], w["ffn2_wd"],
                    w["ple_norm"], w["ple_wg"], w["ple_wp"]]
    return pl.pallas_call(
        _post_kernel,
        grid=grid,
        in_specs=[_row_spec(D), _row_spec(cdim), _row_spec(sdim), _row_spec(p.shape[-1])]
        + [_resident(a.shape) for a in post_weights],
        out_specs=_row_spec(D),
        out_shape=jax.ShapeDtypeStruct((B, S, D), F32),
        scratch_shapes=[pltpu.VMEM((ROW_TILE, d_ff), BF16)],
        compiler_params=dense_params,
        name="post",
    )(x1, yc, ysb, p, *post_weights)


def kernel(x, p, ffn1_norm, ffn1_w_gate, ffn1_w_up, ffn1_w_down, mix_norm, w_in, conv_w, conv_b, q_norm, k_norm, w_out, ffn2_norm, ffn2_w_gate, ffn2_w_up, ffn2_w_down, ple_norm, ple_w_gate, ple_w_proj):
    depth = p.shape[0]
    sdim = w_out.shape[1] - conv_w.shape[2]
    heads = sdim // HEAD_DIM
    head_of = jnp.arange(sdim, dtype=jnp.int32) // HEAD_DIM
    bd = (head_of[:, None] == head_of[None, :]).astype(BF16)
    tri, mask_bias = _attention_constants()
    row = lambda a: a.reshape(1, -1).astype(F32)
    for i in range(depth):
        w = dict(
            ffn1_norm=row(ffn1_norm[i]), ffn1_wg=ffn1_w_gate[i].astype(BF16),
            ffn1_wu=ffn1_w_up[i].astype(BF16), ffn1_wd=ffn1_w_down[i].astype(BF16),
            mix_norm=row(mix_norm[i]), w_in=w_in[i].astype(BF16),
            conv_w=conv_w[i].astype(F32), conv_b=row(conv_b[i]),
            q_gain=row(jnp.tile(q_norm[i], heads)) * (HEAD_DIM ** -0.5 * LOG2_E),
            k_gain=row(jnp.tile(k_norm[i], heads)),
            bd=bd, tri=tri, mask_bias=mask_bias,
            w_out=w_out[i].astype(BF16), ffn2_norm=row(ffn2_norm[i]),
            ffn2_wg=ffn2_w_gate[i].astype(BF16), ffn2_wu=ffn2_w_up[i].astype(BF16),
            ffn2_wd=ffn2_w_down[i].astype(BF16), ple_norm=row(ple_norm[i]),
            ple_wg=ple_w_gate[i].astype(BF16), ple_wp=ple_w_proj[i].astype(BF16))
        x = _layer(x, p[i], w)
    return x
```

```python
import math

import jax
import jax.numpy as jnp
from jax import lax
from jax.experimental import pallas as pl
from jax.experimental.pallas import tpu as pltpu

F32 = jnp.float32
BF16 = jnp.bfloat16

EPS = 1e-6
FFN_RES = 0.5
HEAD_DIM = 64
LANES = 128
HEADS_PER_STEP = LANES // HEAD_DIM
ROW_TILE = 512
FF_CHUNK = 256
ATTN_BLOCK = 256
VMEM_LIMIT_BYTES = 56 * 1024 * 1024
SWEEP_STOP = 160.0
MASK_BIG = 1e30
LOG2_E = math.log2(math.e)
LN_2 = math.log(2.0)


def _rmsnorm(x, g):
    ms = jnp.mean(x * x, axis=-1, keepdims=True)
    return x * lax.rsqrt(ms + EPS) * g


def _split_bf16(x):
    hi = x.astype(BF16)
    lo = (x - hi.astype(F32)).astype(BF16)
    return hi, lo


def _dot(a, b):
    return jnp.dot(a, b, preferred_element_type=F32)


def _dot_nt(a, b):
    return lax.dot_general(a, b, (((1,), (1,)), ((), ())), preferred_element_type=F32)


def _swiglu_half_step(x, g_ref, wg_ref, wu_ref, wd_ref, act_ref):
    h = _rmsnorm(x, g_ref[...]).astype(BF16)
    d_ff = wg_ref.shape[1]
    for c in range(d_ff // FF_CHUNK):
        sl = slice(c * FF_CHUNK, (c + 1) * FF_CHUNK)
        gate = _dot(h, wg_ref[:, sl])
        up = _dot(h, wu_ref[:, sl])
        act_ref[:, sl] = (gate * jax.nn.sigmoid(gate) * up).astype(BF16)
    return x + FFN_RES * _dot(act_ref[...], wd_ref[...])


def _head_rmsnorm(t, gain, bd_ref):
    hi, lo = _split_bf16(t * t)
    ss = _dot(hi, bd_ref[...]) + _dot(lo, bd_ref[...])
    return t * lax.rsqrt(ss * (1.0 / HEAD_DIM) + EPS) * gain


def _store_per_head(dst_ref, t):
    lane = lax.broadcasted_iota(jnp.int32, (t.shape[0], LANES), 1)
    for h in range(t.shape[1] // HEAD_DIM):
        g = h // HEADS_PER_STEP
        group = t[:, g * LANES:(g + 1) * LANES]
        mine = (lane // HEAD_DIM) == (h % HEADS_PER_STEP)
        dst_ref[0, h] = jnp.where(mine, group, 0.0).astype(BF16)


def _pre_kernel(x_ref, g1_ref, wg_ref, wu_ref, wd_ref, gmix_ref, win_ref, cw_ref, cb_ref,
                qg_ref, kg_ref, bd_ref,
                x1_ref, yc_ref, q_ref, k_ref, v_ref,
                act_ref, tail_ref):
    s = pl.program_id(1)
    x1 = _swiglu_half_step(x_ref[0], g1_ref, wg_ref, wu_ref, wd_ref, act_ref)
    x1_ref[0] = x1

    h = _rmsnorm(x1, gmix_ref[...]).astype(BF16)
    rows = h.shape[0]
    cdim = cw_ref.shape[1]

    def proj(i):
        return _dot(h, win_ref[:, i * cdim:(i + 1) * cdim])

    @pl.when(s == 0)
    def _():
        tail_ref[...] = jnp.zeros_like(tail_ref)

    z = proj(1) * proj(2)
    prev1 = tail_ref[7:8, :]
    prev2 = tail_ref[6:7, :]
    row = lax.broadcasted_iota(jnp.int32, z.shape, 0)
    z1 = jnp.where(row == 0, prev1, pltpu.roll(z, 1, 0))
    z2 = jnp.where(row == 0, prev2, jnp.where(row == 1, prev1, pltpu.roll(z, 2, 0)))
    tail_ref[...] = z[rows - 8:, :]
    conv = cw_ref[2:3, :] * z + cw_ref[1:2, :] * z1 + cw_ref[0:1, :] * z2 + cb_ref[...]
    yc_ref[0] = (proj(0) * conv).astype(BF16)

    _store_per_head(q_ref, _head_rmsnorm(proj(3), qg_ref[...], bd_ref))
    k_ref[0] = _head_rmsnorm(proj(4), kg_ref[...], bd_ref).astype(BF16)
    _store_per_head(v_ref, proj(5))


def _attn_kernel(q_ref, k_ref, v_ref, tri_ref, bias_ref, o_ref, z_ref, spb_ref, base_ref, a_ref):
    blk = a_ref.shape[1]
    nq = k_ref.shape[1] // blk

    z_ref[...] = jnp.zeros_like(z_ref)
    spb_ref[...] = jnp.zeros_like(spb_ref)
    base_ref[...] = jnp.zeros_like(base_ref)
    a_ref[...] = jnp.zeros_like(a_ref)

    def rows_of(ref, idx):
        start = pl.multiple_of(idx * blk, blk)
        return jnp.concatenate([ref[0, h, pl.ds(start, blk), :] for h in range(HEADS_PER_STEP)], axis=0)

    def step(state, wr):
        i0, j0, v0, i1, j1, v1, i2, j2, v2, i3, j3, v3, carry, acc = state
        rd = 1 - wr
        settled = jnp.min(carry) > SWEEP_STOP

        pv = _dot(a_ref[rd], rows_of(v_ref, j3))
        acc = jnp.where(v3 == 1, jnp.where(j3 == i3, pv, acc + pv), acc)
        o_ref[0, pl.ds(pl.multiple_of(i3 * blk, blk), blk), :] = acc.astype(BF16)

        cum = _dot(spb_ref[rd], tri_ref[...])
        prob = jnp.exp2(base_ref[rd] - cum).astype(BF16)
        a_ref[wr] = jnp.concatenate([prob[h * blk:(h + 1) * blk] for h in range(HEADS_PER_STEP)], axis=1)

        diag = j1 == i1
        z = jnp.minimum(z_ref[rd], bias_ref[diag.astype(jnp.int32)])
        sp = jnp.maximum(z, 0.0) + jnp.log(1.0 + jnp.exp2(-jnp.abs(z))) * (1.0 / LN_2)
        carry = jnp.where(diag, 0.0, carry)
        base_ref[wr] = (z - sp) - carry
        spb_ref[wr] = sp.astype(BF16)
        carry = carry + jnp.sum(sp, axis=-1, keepdims=True)

        kb = k_ref[0, pl.ds(pl.multiple_of(j0 * blk, blk), blk), :]
        z_ref[wr] = _dot_nt(rows_of(q_ref, i0), kb)

        sweep_done = (j0 == 0) | ((i2 == i0) & (v2 == 1) & settled)
        finished = sweep_done & (i0 == nq - 1)
        i_next = jnp.where(sweep_done & jnp.logical_not(finished), i0 + 1, i0)
        j_next = jnp.where(finished, 0, jnp.where(sweep_done, i0 + 1, j0 - 1))
        v_next = jnp.where(finished, 0, v0)
        return (i_next, j_next, v_next, i0, j0, v0, i1, j1, v1, i2, j2, v2, carry, acc)

    zero = jnp.int32(0)
    init = (zero, zero, jnp.int32(1)) + (zero,) * 9 + (
        jnp.zeros((HEADS_PER_STEP * blk, 1), F32), jnp.zeros((blk, LANES), F32))

    def in_flight(state):
        return (state[2] + state[5] + state[8] + state[11]) > 0

    lax.while_loop(in_flight, lambda st: step(step(st, 0), 1), init)


def _post_kernel(x1_ref, yc_ref, ysb_ref, p_ref, wo_ref, g2_ref, wg_ref, wu_ref, wd_ref,
                 gp_ref, wpg_ref, wpp_ref, o_ref, act_ref):
    cdim = yc_ref.shape[2]
    x2 = x1_ref[0] + _dot(yc_ref[0], wo_ref[:cdim, :]) + _dot(ysb_ref[0], wo_ref[cdim:, :])
    x3 = _swiglu_half_step(x2, g2_ref, wg_ref, wu_ref, wd_ref, act_ref)
    gate = jax.nn.sigmoid(_dot(_rmsnorm(x3, gp_ref[...]).astype(BF16), wpg_ref[...]))
    o_ref[0] = x3 + gate * _dot(p_ref[0].astype(BF16), wpp_ref[...])


def _resident(shape):
    return pl.BlockSpec(shape, lambda *_: (0,) * len(shape), pipeline_mode=pl.Buffered(1))


def _row_spec(width):
    return pl.BlockSpec((1, ROW_TILE, width), lambda b, s: (b, s, 0))


def _attention_constants():
    pos = jnp.arange(ATTN_BLOCK, dtype=jnp.int32)
    tri = (pos[:, None] > pos[None, :]).astype(BF16)
    causal = jnp.tile(pos[None, :] < pos[:, None], (HEADS_PER_STEP, 1))
    mask_bias = jnp.stack([jnp.full(causal.shape, MASK_BIG, F32),
                           jnp.where(causal, MASK_BIG, -MASK_BIG).astype(F32)])
    return tri, mask_bias


def _attention(q, k, v, tri, mask_bias):
    B, heads, S, _ = q.shape
    blk = ATTN_BLOCK
    rows = HEADS_PER_STEP * blk
    pair_spec = pl.BlockSpec((1, HEADS_PER_STEP, S, LANES), lambda b, hp: (b, hp, 0, 0))
    seq_spec = pl.BlockSpec((1, S, LANES), lambda b, hp: (b, 0, hp))
    return pl.pallas_call(
        _attn_kernel,
        grid=(B, heads // HEADS_PER_STEP),
        in_specs=[pair_spec, seq_spec, pair_spec, _resident(tri.shape), _resident(mask_bias.shape)],
        out_specs=seq_spec,
        out_shape=jax.ShapeDtypeStruct(k.shape, BF16),
        scratch_shapes=[pltpu.VMEM((2, rows, blk), F32), pltpu.VMEM((2, rows, blk), BF16),
                        pltpu.VMEM((2, rows, blk), F32), pltpu.VMEM((2, blk, rows), BF16)],
        compiler_params=pltpu.CompilerParams(
            dimension_semantics=("arbitrary", "arbitrary"), vmem_limit_bytes=VMEM_LIMIT_BYTES),
        name="attn",
    )(q, k, v, tri, mask_bias)


def _layer(x, p, w):
    B, S, D = x.shape
    d_ff = w["ffn1_wg"].shape[1]
    cdim = w["conv_w"].shape[1]
    sdim = w["bd"].shape[0]
    heads = sdim // HEAD_DIM
    assert S % ROW_TILE == 0 and S % ATTN_BLOCK == 0 and d_ff % FF_CHUNK == 0
    assert sdim % LANES == 0 and cdim == sdim
    grid = (B, S // ROW_TILE)
    dense_params = pltpu.CompilerParams(
        dimension_semantics=("arbitrary", "arbitrary"), vmem_limit_bytes=VMEM_LIMIT_BYTES)
    per_head_spec = pl.BlockSpec((1, heads, ROW_TILE, LANES), lambda b, s: (b, 0, s, 0))
    per_head_shape = jax.ShapeDtypeStruct((B, heads, S, LANES), BF16)

    pre_weights = [w["ffn1_norm"], w["ffn1_wg"], w["ffn1_wu"], w["ffn1_wd"], w["mix_norm"], w["w_in"],
                   w["conv_w"], w["conv_b"], w["q_gain"], w["k_gain"], w["bd"]]
    x1, yc, q, k, v = pl.pallas_call(
        _pre_kernel,
        grid=grid,
        in_specs=[_row_spec(D)] + [_resident(a.shape) for a in pre_weights],
        out_specs=[_row_spec(D), _row_spec(cdim), per_head_spec, _row_spec(sdim), per_head_spec],
        out_shape=[jax.ShapeDtypeStruct((B, S, D), F32),
                   jax.ShapeDtypeStruct((B, S, cdim), BF16),
                   per_head_shape,
                   jax.ShapeDtypeStruct((B, S, sdim), BF16),
                   per_head_shape],
        scratch_shapes=[pltpu.VMEM((ROW_TILE, d_ff), BF16), pltpu.VMEM((8, cdim), F32)],
        compiler_params=dense_params,
        name="pre",
    )(x, *pre_weights)

    ysb = _attention(q, k, v, w["tri"], w["mask_bias"])

    post_weights = [w["w_out"], w["ffn2_norm"], w["ffn2_wg"], w["ffn2_wu"], w["ffn2_wd"],
                    w["ple_norm"], w["ple_wg"], w["ple_wp"]]
    return pl.pallas_call(
        _post_kernel,
        grid=grid,
        in_specs=[_row_spec(D), _row_spec(cdim), _row_spec(sdim), _row_spec(p.shape[-1])]
        + [_resident(a.shape) for a in post_weights],
        out_specs=_row_spec(D),
        out_shape=jax.ShapeDtypeStruct((B, S, D), F32),
        scratch_shapes=[pltpu.VMEM((ROW_TILE, d_ff), BF16)],
        compiler_params=dense_params,
        name="post",
    )(x1, yc, ysb, p, *post_weights)


def kernel(x, p, ffn1_norm, ffn1_w_gate, ffn1_w_up, ffn1_w_down, mix_norm, w_in, conv_w, conv_b, q_norm, k_norm, w_out, ffn2_norm, ffn2_w_gate, ffn2_w_up, ffn2_w_down, ple_norm, ple_w_gate, ple_w_proj):
    depth = p.shape[0]
    sdim = w_out.shape[1] - conv_w.shape[2]
    heads = sdim // HEAD_DIM
    head_of = jnp.arange(sdim, dtype=jnp.int32) // HEAD_DIM
    bd = (head_of[:, None] == head_of[None, :]).astype(BF16)
    tri, mask_bias = _attention_constants()
    row = lambda a: a.reshape(1, -1).astype(F32)
    for i in range(depth):
        w = dict(
            ffn1_norm=row(ffn1_norm[i]), ffn1_wg=ffn1_w_gate[i].astype(BF16),
            ffn1_wu=ffn1_w_up[i].astype(BF16), ffn1_wd=ffn1_w_down[i].astype(BF16),
            mix_norm=row(mix_norm[i]), w_in=w_in[i].astype(BF16),
            conv_w=conv_w[i].astype(F32), conv_b=row(conv_b[i]),
            q_gain=row(jnp.tile(q_norm[i], heads)) * (HEAD_DIM ** -0.5 * LOG2_E),
            k_gain=row(jnp.tile(k_norm[i], heads)),
            bd=bd, tri=tri, mask_bias=mask_bias,
            w_out=w_out[i].astype(BF16), ffn2_norm=row(ffn2_norm[i]),
            ffn2_wg=ffn2_w_gate[i].astype(BF16), ffn2_wu=ffn2_w_up[i].astype(BF16),
            ffn2_wd=ffn2_w_down[i].astype(BF16), ple_norm=row(ple_norm[i]),
            ple_wg=ple_w_gate[i].astype(BF16), ple_wp=ple_w_proj[i].astype(BF16))
        x = _layer(x, p[i], w)
    return x
```

```python
import math

import jax
import jax.numpy as jnp
from jax import lax
from jax.experimental import pallas as pl
from jax.experimental.pallas import tpu as pltpu

F32 = jnp.float32
BF16 = jnp.bfloat16

EPS = 1e-6
FFN_RES = 0.5
HEAD_DIM = 64
LANES = 128
HEADS_PER_STEP = LANES // HEAD_DIM
ROW_TILE = 512
FF_CHUNK = 256
ATTN_BLOCK = 256
VMEM_LIMIT_BYTES = 56 * 1024 * 1024
SWEEP_STOP = 160.0
MASK_BIG = 1e30
LOG2_E = math.log2(math.e)
LN_2 = math.log(2.0)


def _rmsnorm(x, g):
    ms = jnp.mean(x * x, axis=-1, keepdims=True)
    return x * lax.rsqrt(ms + EPS) * g


def _split_bf16(x):
    hi = x.astype(BF16)
    lo = (x - hi.astype(F32)).astype(BF16)
    return hi, lo


def _dot(a, b):
    return jnp.dot(a, b, preferred_element_type=F32)


def _dot_nt(a, b):
    return lax.dot_general(a, b, (((1,), (1,)), ((), ())), preferred_element_type=F32)


def _swiglu_half_step(x, g_ref, wg_ref, wu_ref, wd_ref, act_ref):
    h = _rmsnorm(x, g_ref[...]).astype(BF16)
    d_ff = wg_ref.shape[1]
    for c in range(d_ff // FF_CHUNK):
        sl = slice(c * FF_CHUNK, (c + 1) * FF_CHUNK)
        gate = _dot(h, wg_ref[:, sl])
        up = _dot(h, wu_ref[:, sl])
        act_ref[:, sl] = (gate * jax.nn.sigmoid(gate) * up).astype(BF16)
    return x + FFN_RES * _dot(act_ref[...], wd_ref[...])


def _head_rmsnorm(t, gain, bd_ref):
    hi, lo = _split_bf16(t * t)
    ss = _dot(hi, bd_ref[...]) + _dot(lo, bd_ref[...])
    return t * lax.rsqrt(ss * (1.0 / HEAD_DIM) + EPS) * gain


def _store_per_head(dst_ref, t):
    lane = lax.broadcasted_iota(jnp.int32, (t.shape[0], LANES), 1)
    for h in range(t.shape[1] // HEAD_DIM):
        g = h // HEADS_PER_STEP
        group = t[:, g * LANES:(g + 1) * LANES]
        mine = (lane // HEAD_DIM) == (h % HEADS_PER_STEP)
        dst_ref[0, h] = jnp.where(mine, group, 0.0).astype(BF16)


def _pre_kernel(x_ref, g1_ref, wg_ref, wu_ref, wd_ref, gmix_ref, win_ref, cw_ref, cb_ref,
                qg_ref, kg_ref, bd_ref,
                x1_ref, yc_ref, q_ref, k_ref, v_ref,
                act_ref, tail_ref):
    s = pl.program_id(1)
    x1 = _swiglu_half_step(x_ref[0], g1_ref, wg_ref, wu_ref, wd_ref, act_ref)
    x1_ref[0] = x1

    h = _rmsnorm(x1, gmix_ref[...]).astype(BF16)
    rows = h.shape[0]
    cdim = cw_ref.shape[1]

    def proj(i):
        return _dot(h, win_ref[:, i * cdim:(i + 1) * cdim])

    @pl.when(s == 0)
    def _():
        tail_ref[...] = jnp.zeros_like(tail_ref)

    z = proj(1) * proj(2)
    prev1 = tail_ref[7:8, :]
    prev2 = tail_ref[6:7, :]
    row = lax.broadcasted_iota(jnp.int32, z.shape, 0)
    z1 = jnp.where(row == 0, prev1, pltpu.roll(z, 1, 0))
    z2 = jnp.where(row == 0, prev2, jnp.where(row == 1, prev1, pltpu.roll(z, 2, 0)))
    tail_ref[...] = z[rows - 8:, :]
    conv = cw_ref[2:3, :] * z + cw_ref[1:2, :] * z1 + cw_ref[0:1, :] * z2 + cb_ref[...]
    yc_ref[0] = (proj(0) * conv).astype(BF16)

    _store_per_head(q_ref, _head_rmsnorm(proj(3), qg_ref[...], bd_ref))
    k_ref[0] = _head_rmsnorm(proj(4), kg_ref[...], bd_ref).astype(BF16)
    _store_per_head(v_ref, proj(5))


def _attn_kernel(q_ref, k_ref, v_ref, tri_ref, bias_ref, o_ref, z_ref, spb_ref, base_ref, a_ref):
    blk = a_ref.shape[1]
    nq = k_ref.shape[1] // blk

    z_ref[...] = jnp.zeros_like(z_ref)
    spb_ref[...] = jnp.zeros_like(spb_ref)
    base_ref[...] = jnp.zeros_like(base_ref)
    a_ref[...] = jnp.zeros_like(a_ref)

    def rows_of(ref, idx):
        start = pl.multiple_of(idx * blk, blk)
        return jnp.concatenate([ref[0, h, pl.ds(start, blk), :] for h in range(HEADS_PER_STEP)], axis=0)

    def step(state, wr):
        i0, j0, v0, i1, j1, v1, i2, j2, v2, i3, j3, v3, carry, acc = state
        rd = 1 - wr

        pv = _dot(a_ref[rd], rows_of(v_ref, j3))
        acc = jnp.where(v3 == 1, jnp.where(j3 == i3, pv, acc + pv), acc)
        o_ref[0, pl.ds(pl.multiple_of(i3 * blk, blk), blk), :] = acc.astype(BF16)

        cum = _dot(spb_ref[rd], tri_ref[...])
        prob = jnp.exp2(base_ref[rd] - cum).astype(BF16)
        a_ref[wr] = jnp.concatenate([prob[h * blk:(h + 1) * blk] for h in range(HEADS_PER_STEP)], axis=1)

        diag = j1 == i1
        z = jnp.minimum(z_ref[rd], bias_ref[diag.astype(jnp.int32)])
        sp = jnp.maximum(z, 0.0) + jnp.log(1.0 + jnp.exp2(-jnp.abs(z))) * (1.0 / LN_2)
        carry = jnp.where(diag, 0.0, carry)
        base_ref[wr] = (z - sp) - carry
        spb_ref[wr] = sp.astype(BF16)
        carry = carry + jnp.sum(sp, axis=-1, keepdims=True)
        settled = jnp.min(carry) > SWEEP_STOP

        kb = k_ref[0, pl.ds(pl.multiple_of(j0 * blk, blk), blk), :]
        z_ref[wr] = _dot_nt(rows_of(q_ref, i0), kb)

        sweep_done = (j0 == 0) | ((i1 == i0) & (v1 == 1) & settled)
        finished = sweep_done & (i0 == nq - 1)
        i_next = jnp.where(sweep_done & jnp.logical_not(finished), i0 + 1, i0)
        j_next = jnp.where(finished, 0, jnp.where(sweep_done, i0 + 1, j0 - 1))
        v_next = jnp.where(finished, 0, v0)
        return (i_next, j_next, v_next, i0, j0, v0, i1, j1, v1, i2, j2, v2, carry, acc)

    zero = jnp.int32(0)
    init = (zero, zero, jnp.int32(1)) + (zero,) * 9 + (
        jnp.zeros((HEADS_PER_STEP * blk, 1), F32), jnp.zeros((blk, LANES), F32))

    def in_flight(state):
        return (state[2] + state[5] + state[8] + state[11]) > 0

    lax.while_loop(in_flight, lambda st: step(step(st, 0), 1), init)


def _post_kernel(x1_ref, yc_ref, ysb_ref, p_ref, wo_ref, g2_ref, wg_ref, wu_ref, wd_ref,
                 gp_ref, wpg_ref, wpp_ref, o_ref, act_ref):
    cdim = yc_ref.shape[2]
    x2 = x1_ref[0] + _dot(yc_ref[0], wo_ref[:cdim, :]) + _dot(ysb_ref[0], wo_ref[cdim:, :])
    x3 = _swiglu_half_step(x2, g2_ref, wg_ref, wu_ref, wd_ref, act_ref)
    gate = jax.nn.sigmoid(_dot(_rmsnorm(x3, gp_ref[...]).astype(BF16), wpg_ref[...]))
    o_ref[0] = x3 + gate * _dot(p_ref[0].astype(BF16), wpp_ref[...])


def _resident(shape):
    return pl.BlockSpec(shape, lambda *_: (0,) * len(shape), pipeline_mode=pl.Buffered(1))


def _row_spec(width):
    return pl.BlockSpec((1, ROW_TILE, width), lambda b, s: (b, s, 0))


def _attention_constants():
    pos = jnp.arange(ATTN_BLOCK, dtype=jnp.int32)
    tri = (pos[:, None] > pos[None, :]).astype(BF16)
    causal = jnp.tile(pos[None, :] < pos[:, None], (HEADS_PER_STEP, 1))
    mask_bias = jnp.stack([jnp.full(causal.shape, MASK_BIG, F32),
                           jnp.where(causal, MASK_BIG, -MASK_BIG).astype(F32)])
    return tri, mask_bias


def _attention(q, k, v, tri, mask_bias):
    B, heads, S, _ = q.shape
    blk = ATTN_BLOCK
    rows = HEADS_PER_STEP * blk
    pair_spec = pl.BlockSpec((1, HEADS_PER_STEP, S, LANES), lambda b, hp: (b, hp, 0, 0))
    seq_spec = pl.BlockSpec((1, S, LANES), lambda b, hp: (b, 0, hp))
    return pl.pallas_call(
        _attn_kernel,
        grid=(B, heads // HEADS_PER_STEP),
        in_specs=[pair_spec, seq_spec, pair_spec, _resident(tri.shape), _resident(mask_bias.shape)],
        out_specs=seq_spec,
        out_shape=jax.ShapeDtypeStruct(k.shape, BF16),
        scratch_shapes=[pltpu.VMEM((2, rows, blk), F32), pltpu.VMEM((2, rows, blk), BF16),
                        pltpu.VMEM((2, rows, blk), F32), pltpu.VMEM((2, blk, rows), BF16)],
        compiler_params=pltpu.CompilerParams(
            dimension_semantics=("arbitrary", "arbitrary"), vmem_limit_bytes=VMEM_LIMIT_BYTES),
        name="attn",
    )(q, k, v, tri, mask_bias)


def _layer(x, p, w):
    B, S, D = x.shape
    d_ff = w["ffn1_wg"].shape[1]
    cdim = w["conv_w"].shape[1]
    sdim = w["bd"].shape[0]
    heads = sdim // HEAD_DIM
    assert S % ROW_TILE == 0 and S % ATTN_BLOCK == 0 and d_ff % FF_CHUNK == 0
    assert sdim % LANES == 0 and cdim == sdim
    grid = (B, S // ROW_TILE)
    dense_params = pltpu.CompilerParams(
        dimension_semantics=("arbitrary", "arbitrary"), vmem_limit_bytes=VMEM_LIMIT_BYTES)
    per_head_spec = pl.BlockSpec((1, heads, ROW_TILE, LANES), lambda b, s: (b, 0, s, 0))
    per_head_shape = jax.ShapeDtypeStruct((B, heads, S, LANES), BF16)

    pre_weights = [w["ffn1_norm"], w["ffn1_wg"], w["ffn1_wu"], w["ffn1_wd"], w["mix_norm"], w["w_in"],
                   w["conv_w"], w["conv_b"], w["q_gain"], w["k_gain"], w["bd"]]
    x1, yc, q, k, v = pl.pallas_call(
        _pre_kernel,
        grid=grid,
        in_specs=[_row_spec(D)] + [_resident(a.shape) for a in pre_weights],
        out_specs=[_row_spec(D), _row_spec(cdim), per_head_spec, _row_spec(sdim), per_head_spec],
        out_shape=[jax.ShapeDtypeStruct((B, S, D), F32),
                   jax.ShapeDtypeStruct((B, S, cdim), BF16),
                   per_head_shape,
                   jax.ShapeDtypeStruct((B, S, sdim), BF16),
                   per_head_shape],
        scratch_shapes=[pltpu.VMEM((ROW_TILE, d_ff), BF16), pltpu.VMEM((8, cdim), F32)],
        compiler_params=dense_params,
        name="pre",
    )(x, *pre_weights)

    ysb = _attention(q, k, v, w["tri"], w["mask_bias"])

    post_weights = [w["w_out"], w["ffn2_norm"], w["ffn2_wg"], w["ffn2_wu"], w["ffn2_wd"],
                    w["ple_norm"], w["ple_wg"], w["ple_wp"]]
    return pl.pallas_call(
        _post_kernel,
        grid=grid,
        in_specs=[_row_spec(D), _row_spec(cdim), _row_spec(sdim), _row_spec(p.shape[-1])]
        + [_resident(a.shape) for a in post_weights],
        out_specs=_row_spec(D),
        out_shape=jax.ShapeDtypeStruct((B, S, D), F32),
        scratch_shapes=[pltpu.VMEM((ROW_TILE, d_ff), BF16)],
        compiler_params=dense_params,
        name="post",
    )(x1, yc, ysb, p, *post_weights)


def kernel(x, p, ffn1_norm, ffn1_w_gate, ffn1_w_up, ffn1_w_down, mix_norm, w_in, conv_w, conv_b, q_norm, k_norm, w_out, ffn2_norm, ffn2_w_gate, ffn2_w_up, ffn2_w_down, ple_norm, ple_w_gate, ple_w_proj):
    depth = p.shape[0]
    sdim = w_out.shape[1] - conv_w.shape[2]
    heads = sdim // HEAD_DIM
    head_of = jnp.arange(sdim, dtype=jnp.int32) // HEAD_DIM
    bd = (head_of[:, None] == head_of[None, :]).astype(BF16)
    tri, mask_bias = _attention_constants()
    row = lambda a: a.reshape(1, -1).astype(F32)
    for i in range(depth):
        w = dict(
            ffn1_norm=row(ffn1_norm[i]), ffn1_wg=ffn1_w_gate[i].astype(BF16),
            ffn1_wu=ffn1_w_up[i].astype(BF16), ffn1_wd=ffn1_w_down[i].astype(BF16),
            mix_norm=row(mix_norm[i]), w_in=w_in[i].astype(BF16),
            conv_w=conv_w[i].astype(F32), conv_b=row(conv_b[i]),
            q_gain=row(jnp.tile(q_norm[i], heads)) * (HEAD_DIM ** -0.5 * LOG2_E),
            k_gain=row(jnp.tile(k_norm[i], heads)),
            bd=bd, tri=tri, mask_bias=mask_bias,
            w_out=w_out[i].astype(BF16), ffn2_norm=row(ffn2_norm[i]),
            ffn2_wg=ffn2_w_gate[i].astype(BF16), ffn2_wu=ffn2_w_up[i].astype(BF16),
            ffn2_wd=ffn2_w_down[i].astype(BF16), ple_norm=row(ple_norm[i]),
            ple_wg=ple_w_gate[i].astype(BF16), ple_wp=ple_w_proj[i].astype(BF16))
        x = _layer(x, p[i], w)
    return x
```

```python
import math

import jax
import jax.numpy as jnp
from jax import lax
from jax.experimental import pallas as pl
from jax.experimental.pallas import tpu as pltpu

F32 = jnp.float32
BF16 = jnp.bfloat16

EPS = 1e-6
FFN_RES = 0.5
HEAD_DIM = 64
LANES = 128
HEADS_PER_STEP = LANES // HEAD_DIM
ROW_TILE = 512
FF_CHUNK = 256
ATTN_BLOCK = 256
VMEM_LIMIT_BYTES = 56 * 1024 * 1024
SWEEP_STOP = 160.0
MASK_BIG = 1e30
LOG2_E = math.log2(math.e)
LN_2 = math.log(2.0)


def _rmsnorm(x, g):
    ms = jnp.mean(x * x, axis=-1, keepdims=True)
    return x * lax.rsqrt(ms + EPS) * g


def _dot(a, b):
    return jnp.dot(a, b, preferred_element_type=F32)


def _dot_nt(a, b):
    return lax.dot_general(a, b, (((1,), (1,)), ((), ())), preferred_element_type=F32)


def _swiglu_half_step(x, g_ref, wg_ref, wu_ref, wd_ref, act_ref):
    h = _rmsnorm(x, g_ref[...]).astype(BF16)
    d_ff = wg_ref.shape[1]
    for c in range(d_ff // FF_CHUNK):
        sl = slice(c * FF_CHUNK, (c + 1) * FF_CHUNK)
        gate = _dot(h, wg_ref[:, sl])
        up = _dot(h, wu_ref[:, sl])
        act_ref[:, sl] = (gate * jax.nn.sigmoid(gate) * up).astype(BF16)
    return x + FFN_RES * _dot(act_ref[...], wd_ref[...])


def _head_rmsnorm(t, gain, bd_ref):
    ss = _dot((t * t).astype(BF16), bd_ref[...])
    return t * lax.rsqrt(ss * (1.0 / HEAD_DIM) + EPS) * gain


def _store_per_head(dst_ref, t):
    lane = lax.broadcasted_iota(jnp.int32, (t.shape[0], LANES), 1)
    for h in range(t.shape[1] // HEAD_DIM):
        g = h // HEADS_PER_STEP
        group = t[:, g * LANES:(g + 1) * LANES]
        mine = (lane // HEAD_DIM) == (h % HEADS_PER_STEP)
        dst_ref[0, h] = jnp.where(mine, group, 0.0).astype(BF16)


def _pre_kernel(x_ref, g1_ref, wg_ref, wu_ref, wd_ref, gmix_ref, win_ref, cw_ref, cb_ref,
                qg_ref, kg_ref, bd_ref,
                x1_ref, yc_ref, q_ref, k_ref, v_ref,
                act_ref, tail_ref):
    s = pl.program_id(1)
    x1 = _swiglu_half_step(x_ref[0], g1_ref, wg_ref, wu_ref, wd_ref, act_ref)
    x1_ref[0] = x1

    h = _rmsnorm(x1, gmix_ref[...]).astype(BF16)
    rows = h.shape[0]
    cdim = cw_ref.shape[1]

    def proj(i):
        return _dot(h, win_ref[:, i * cdim:(i + 1) * cdim])

    @pl.when(s == 0)
    def _():
        tail_ref[...] = jnp.zeros_like(tail_ref)

    z = proj(1) * proj(2)
    prev1 = tail_ref[7:8, :]
    prev2 = tail_ref[6:7, :]
    row = lax.broadcasted_iota(jnp.int32, z.shape, 0)
    z1 = jnp.where(row == 0, prev1, pltpu.roll(z, 1, 0))
    z2 = jnp.where(row == 0, prev2, jnp.where(row == 1, prev1, pltpu.roll(z, 2, 0)))
    tail_ref[...] = z[rows - 8:, :]
    conv = cw_ref[2:3, :] * z + cw_ref[1:2, :] * z1 + cw_ref[0:1, :] * z2 + cb_ref[...]
    yc_ref[0] = (proj(0) * conv).astype(BF16)

    _store_per_head(q_ref, _head_rmsnorm(proj(3), qg_ref[...], bd_ref))
    k_ref[0] = _head_rmsnorm(proj(4), kg_ref[...], bd_ref).astype(BF16)
    _store_per_head(v_ref, proj(5))


def _attn_kernel(q_ref, k_ref, v_ref, tri_ref, bias_ref, o_ref, z_ref, spb_ref, base_ref, a_ref):
    blk = a_ref.shape[1]
    nq = k_ref.shape[1] // blk

    z_ref[...] = jnp.zeros_like(z_ref)
    spb_ref[...] = jnp.zeros_like(spb_ref)
    base_ref[...] = jnp.zeros_like(base_ref)
    a_ref[...] = jnp.zeros_like(a_ref)

    def rows_of(ref, idx):
        start = pl.multiple_of(idx * blk, blk)
        return jnp.concatenate([ref[0, h, pl.ds(start, blk), :] for h in range(HEADS_PER_STEP)], axis=0)

    def step(state, wr):
        i0, j0, v0, i1, j1, v1, i2, j2, v2, i3, j3, v3, carry, acc = state
        rd = 1 - wr

        pv = _dot(a_ref[rd], rows_of(v_ref, j3))
        acc = jnp.where(v3 == 1, jnp.where(j3 == i3, pv, acc + pv), acc)
        o_ref[0, pl.ds(pl.multiple_of(i3 * blk, blk), blk), :] = acc.astype(BF16)

        cum = _dot(spb_ref[rd], tri_ref[...])
        prob = jnp.exp2(base_ref[rd] - cum).astype(BF16)
        a_ref[wr] = jnp.concatenate([prob[h * blk:(h + 1) * blk] for h in range(HEADS_PER_STEP)], axis=1)

        diag = j1 == i1
        z = jnp.minimum(z_ref[rd], bias_ref[diag.astype(jnp.int32)])
        sp = jnp.maximum(z, 0.0) + jnp.log(1.0 + jnp.exp2(-jnp.abs(z))) * (1.0 / LN_2)
        carry = jnp.where(diag, 0.0, carry)
        base_ref[wr] = (z - sp) - carry
        spb_ref[wr] = sp.astype(BF16)
        carry = carry + jnp.sum(sp, axis=-1, keepdims=True)
        settled = jnp.min(carry) > SWEEP_STOP

        kb = k_ref[0, pl.ds(pl.multiple_of(j0 * blk, blk), blk), :]
        z_ref[wr] = _dot_nt(rows_of(q_ref, i0), kb)

        sweep_done = (j0 == 0) | ((i1 == i0) & (v1 == 1) & settled)
        finished = sweep_done & (i0 == nq - 1)
        i_next = jnp.where(sweep_done & jnp.logical_not(finished), i0 + 1, i0)
        j_next = jnp.where(finished, 0, jnp.where(sweep_done, i0 + 1, j0 - 1))
        v_next = jnp.where(finished, 0, v0)
        return (i_next, j_next, v_next, i0, j0, v0, i1, j1, v1, i2, j2, v2, carry, acc)

    zero = jnp.int32(0)
    init = (zero, zero, jnp.int32(1)) + (zero,) * 9 + (
        jnp.zeros((HEADS_PER_STEP * blk, 1), F32), jnp.zeros((blk, LANES), F32))

    def in_flight(state):
        return (state[2] + state[5] + state[8] + state[11]) > 0

    lax.while_loop(in_flight, lambda st: step(step(st, 0), 1), init)


def _post_kernel(x1_ref, yc_ref, ysb_ref, p_ref, wo_ref, g2_ref, wg_ref, wu_ref, wd_ref,
                 gp_ref, wpg_ref, wpp_ref, o_ref, act_ref):
    cdim = yc_ref.shape[2]
    x2 = x1_ref[0] + _dot(yc_ref[0], wo_ref[:cdim, :]) + _dot(ysb_ref[0], wo_ref[cdim:, :])
    x3 = _swiglu_half_step(x2, g2_ref, wg_ref, wu_ref, wd_ref, act_ref)
    gate = jax.nn.sigmoid(_dot(_rmsnorm(x3, gp_ref[...]).astype(BF16), wpg_ref[...]))
    o_ref[0] = x3 + gate * _dot(p_ref[0].astype(BF16), wpp_ref[...])


def _resident(shape):
    return pl.BlockSpec(shape, lambda *_: (0,) * len(shape), pipeline_mode=pl.Buffered(1))


def _row_spec(width):
    return pl.BlockSpec((1, ROW_TILE, width), lambda b, s: (b, s, 0))


def _attention_constants():
    pos = jnp.arange(ATTN_BLOCK, dtype=jnp.int32)
    tri = (pos[:, None] > pos[None, :]).astype(BF16)
    causal = jnp.tile(pos[None, :] < pos[:, None], (HEADS_PER_STEP, 1))
    mask_bias = jnp.stack([jnp.full(causal.shape, MASK_BIG, F32),
                           jnp.where(causal, MASK_BIG, -MASK_BIG).astype(F32)])
    return tri, mask_bias


def _attention(q, k, v, tri, mask_bias):
    B, heads, S, _ = q.shape
    blk = ATTN_BLOCK
    rows = HEADS_PER_STEP * blk
    pair_spec = pl.BlockSpec((1, HEADS_PER_STEP, S, LANES), lambda b, hp: (b, hp, 0, 0))
    seq_spec = pl.BlockSpec((1, S, LANES), lambda b, hp: (b, 0, hp))
    return pl.pallas_call(
        _attn_kernel,
        grid=(B, heads // HEADS_PER_STEP),
        in_specs=[pair_spec, seq_spec, pair_spec, _resident(tri.shape), _resident(mask_bias.shape)],
        out_specs=seq_spec,
        out_shape=jax.ShapeDtypeStruct(k.shape, BF16),
        scratch_shapes=[pltpu.VMEM((2, rows, blk), F32), pltpu.VMEM((2, rows, blk), BF16),
                        pltpu.VMEM((2, rows, blk), F32), pltpu.VMEM((2, blk, rows), BF16)],
        compiler_params=pltpu.CompilerParams(
            dimension_semantics=("arbitrary", "arbitrary"), vmem_limit_bytes=VMEM_LIMIT_BYTES),
        name="attn",
    )(q, k, v, tri, mask_bias)


def _layer(x, p, w):
    B, S, D = x.shape
    d_ff = w["ffn1_wg"].shape[1]
    cdim = w["conv_w"].shape[1]
    sdim = w["bd"].shape[0]
    heads = sdim // HEAD_DIM
    assert S % ROW_TILE == 0 and S % ATTN_BLOCK == 0 and d_ff % FF_CHUNK == 0
    assert sdim % LANES == 0 and cdim == sdim
    grid = (B, S // ROW_TILE)
    dense_params = pltpu.CompilerParams(
        dimension_semantics=("arbitrary", "arbitrary"), vmem_limit_bytes=VMEM_LIMIT_BYTES)
    per_head_spec = pl.BlockSpec((1, heads, ROW_TILE, LANES), lambda b, s: (b, 0, s, 0))
    per_head_shape = jax.ShapeDtypeStruct((B, heads, S, LANES), BF16)

    pre_weights = [w["ffn1_norm"], w["ffn1_wg"], w["ffn1_wu"], w["ffn1_wd"], w["mix_norm"], w["w_in"],
                   w["conv_w"], w["conv_b"], w["q_gain"], w["k_gain"], w["bd"]]
    x1, yc, q, k, v = pl.pallas_call(
        _pre_kernel,
        grid=grid,
        in_specs=[_row_spec(D)] + [_resident(a.shape) for a in pre_weights],
        out_specs=[_row_spec(D), _row_spec(cdim), per_head_spec, _row_spec(sdim), per_head_spec],
        out_shape=[jax.ShapeDtypeStruct((B, S, D), F32),
                   jax.ShapeDtypeStruct((B, S, cdim), BF16),
                   per_head_shape,
                   jax.ShapeDtypeStruct((B, S, sdim), BF16),
                   per_head_shape],
        scratch_shapes=[pltpu.VMEM((ROW_TILE, d_ff), BF16), pltpu.VMEM((8, cdim), F32)],
        compiler_params=dense_params,
        name="pre",
    )(x, *pre_weights)

    ysb = _attention(q, k, v, w["tri"], w["mask_bias"])

    post_weights = [w["w_out"], w["ffn2_norm"], w["ffn2_wg"], w["ffn2_wu"], w["ffn2_wd"],
                    w["ple_norm"], w["ple_wg"], w["ple_wp"]]
    return pl.pallas_call(
        _post_kernel,
        grid=grid,
        in_specs=[_row_spec(D), _row_spec(cdim), _row_spec(sdim), _row_spec(p.shape[-1])]
        + [_resident(a.shape) for a in post_weights],
        out_specs=_row_spec(D),
        out_shape=jax.ShapeDtypeStruct((B, S, D), F32),
        scratch_shapes=[pltpu.VMEM((ROW_TILE, d_ff), BF16)],
        compiler_params=dense_params,
        name="post",
    )(x1, yc, ysb, p, *post_weights)


def kernel(x, p, ffn1_norm, ffn1_w_gate, ffn1_w_up, ffn1_w_down, mix_norm, w_in, conv_w, conv_b, q_norm, k_norm, w_out, ffn2_norm, ffn2_w_gate, ffn2_w_up, ffn2_w_down, ple_norm, ple_w_gate, ple_w_proj):
    depth = p.shape[0]
    sdim = w_out.shape[1] - conv_w.shape[2]
    heads = sdim // HEAD_DIM
    head_of = jnp.arange(sdim, dtype=jnp.int32) // HEAD_DIM
    bd = (head_of[:, None] == head_of[None, :]).astype(BF16)
    tri, mask_bias = _attention_constants()
    row = lambda a: a.reshape(1, -1).astype(F32)
    for i in range(depth):
        w = dict(
            ffn1_norm=row(ffn1_norm[i]), ffn1_wg=ffn1_w_gate[i].astype(BF16),
            ffn1_wu=ffn1_w_up[i].astype(BF16), ffn1_wd=ffn1_w_down[i].astype(BF16),
            mix_norm=row(mix_norm[i]), w_in=w_in[i].astype(BF16),
            conv_w=conv_w[i].astype(F32), conv_b=row(conv_b[i]),
            q_gain=row(jnp.tile(q_norm[i], heads)) * (HEAD_DIM ** -0.5 * LOG2_E),
            k_gain=row(jnp.tile(k_norm[i], heads)),
            bd=bd, tri=tri, mask_bias=mask_bias,
            w_out=w_out[i].astype(BF16), ffn2_norm=row(ffn2_norm[i]),
            ffn2_wg=ffn2_w_gate[i].astype(BF16), ffn2_wu=ffn2_w_up[i].astype(BF16),
            ffn2_wd=ffn2_w_down[i].astype(BF16), ple_norm=row(ple_norm[i]),
            ple_wg=ple_w_gate[i].astype(BF16), ple_wp=ple_w_proj[i].astype(BF16))
        x = _layer(x, p[i], w)
    return x
```

```python
import math

import jax
import jax.numpy as jnp
from jax import lax
from jax.experimental import pallas as pl
from jax.experimental.pallas import tpu as pltpu

F32 = jnp.float32
BF16 = jnp.bfloat16

EPS = 1e-6
FFN_RES = 0.5
HEAD_DIM = 64
LANES = 128
HEADS_PER_STEP = LANES // HEAD_DIM
ROW_TILE = 512
FF_CHUNK = 256
ATTN_BLOCK = 256
VMEM_LIMIT_BYTES = 56 * 1024 * 1024
SWEEP_STOP = 160.0
MASK_BIG = 1e30
LOG2_E = math.log2(math.e)
LN_2 = math.log(2.0)


def _rmsnorm(x, g):
    ms = jnp.mean(x * x, axis=-1, keepdims=True)
    return x * lax.rsqrt(ms + EPS) * g


def _dot(a, b):
    return jnp.dot(a, b, preferred_element_type=F32)


def _dot_nt(a, b):
    return lax.dot_general(a, b, (((1,), (1,)), ((), ())), preferred_element_type=F32)


def _swiglu_half_step(x, g_ref, wg_ref, wu_ref, wd_ref, act_ref):
    h = _rmsnorm(x, g_ref[...]).astype(BF16)
    d_ff = wg_ref.shape[1]
    for c in range(d_ff // FF_CHUNK):
        sl = slice(c * FF_CHUNK, (c + 1) * FF_CHUNK)
        gate = _dot(h, wg_ref[:, sl])
        up = _dot(h, wu_ref[:, sl])
        act_ref[:, sl] = (gate * jax.nn.sigmoid(gate) * up).astype(BF16)
    return x + FFN_RES * _dot(act_ref[...], wd_ref[...])


def _head_rmsnorm(t, gain, bd_ref):
    ss = _dot((t * t).astype(BF16), bd_ref[...])
    return t * lax.rsqrt(ss * (1.0 / HEAD_DIM) + EPS) * gain


def _store_per_head(dst_ref, t):
    lane = lax.broadcasted_iota(jnp.int32, (t.shape[0], LANES), 1)
    for h in range(t.shape[1] // HEAD_DIM):
        g = h // HEADS_PER_STEP
        group = t[:, g * LANES:(g + 1) * LANES]
        mine = (lane // HEAD_DIM) == (h % HEADS_PER_STEP)
        dst_ref[0, h] = jnp.where(mine, group, 0.0).astype(BF16)


def _pre_kernel(x_ref, g1_ref, wg_ref, wu_ref, wd_ref, gmix_ref, win_ref, cw_ref, cb_ref,
                qg_ref, kg_ref, bd_ref,
                x1_ref, yc_ref, q_ref, k_ref, v_ref,
                act_ref, tail_ref):
    s = pl.program_id(1)
    x1 = _swiglu_half_step(x_ref[0], g1_ref, wg_ref, wu_ref, wd_ref, act_ref)
    x1_ref[0] = x1

    h = _rmsnorm(x1, gmix_ref[...]).astype(BF16)
    rows = h.shape[0]
    cdim = cw_ref.shape[1]

    def proj(i):
        return _dot(h, win_ref[:, i * cdim:(i + 1) * cdim])

    @pl.when(s == 0)
    def _():
        tail_ref[...] = jnp.zeros_like(tail_ref)

    z = proj(1) * proj(2)
    prev1 = tail_ref[7:8, :]
    prev2 = tail_ref[6:7, :]
    row = lax.broadcasted_iota(jnp.int32, z.shape, 0)
    z1 = jnp.where(row == 0, prev1, pltpu.roll(z, 1, 0))
    z2 = jnp.where(row == 0, prev2, jnp.where(row == 1, prev1, pltpu.roll(z, 2, 0)))
    tail_ref[...] = z[rows - 8:, :]
    conv = cw_ref[2:3, :] * z + cw_ref[1:2, :] * z1 + cw_ref[0:1, :] * z2 + cb_ref[...]
    yc_ref[0] = (proj(0) * conv).astype(BF16)

    _store_per_head(q_ref, _head_rmsnorm(proj(3), qg_ref[...], bd_ref))
    k_ref[0] = _head_rmsnorm(proj(4), kg_ref[...], bd_ref).astype(BF16)
    _store_per_head(v_ref, proj(5))


def _attn_kernel(q_ref, k_ref, v_ref, tri_ref, bias_ref, o_ref, z_ref, spb_ref, base_ref, a_ref):
    blk = a_ref.shape[1]
    nq = k_ref.shape[1] // blk

    z_ref[...] = jnp.zeros_like(z_ref)
    spb_ref[...] = jnp.zeros_like(spb_ref)
    base_ref[...] = jnp.zeros_like(base_ref)
    a_ref[...] = jnp.zeros_like(a_ref)

    def rows_of(ref, idx):
        start = pl.multiple_of(idx * blk, blk)
        return jnp.concatenate([ref[0, h, pl.ds(start, blk), :] for h in range(HEADS_PER_STEP)], axis=0)

    def step(state, lane):
        pend, p1, p2, p3, carries, accs = state
        other = 1 - lane
        wr, rd = lane, other
        i0, j0, v0 = pend[lane]
        i1, j1, v1 = p1
        i3, j3, v3 = p3

        pv = _dot(a_ref[rd], rows_of(v_ref, j3))
        acc = jnp.where(v3 == 1, jnp.where(j3 == i3, pv, accs[other] + pv), accs[other])
        o_ref[0, pl.ds(pl.multiple_of(i3 * blk, blk), blk), :] = acc.astype(BF16)

        cum = _dot(spb_ref[rd], tri_ref[...])
        prob = jnp.exp2(base_ref[rd] - cum).astype(BF16)
        a_ref[wr] = jnp.concatenate([prob[h * blk:(h + 1) * blk] for h in range(HEADS_PER_STEP)], axis=1)

        diag = j1 == i1
        z = jnp.minimum(z_ref[rd], bias_ref[diag.astype(jnp.int32)])
        sp = jnp.maximum(z, 0.0) + jnp.log(1.0 + jnp.exp2(-jnp.abs(z))) * (1.0 / LN_2)
        carry = jnp.where(diag, 0.0, carries[other])
        base_ref[wr] = (z - sp) - carry
        spb_ref[wr] = sp.astype(BF16)
        carry = carry + jnp.sum(sp, axis=-1, keepdims=True)
        sweep_done = (j1 == 0) | (jnp.min(carry) > SWEEP_STOP)
        finished = sweep_done & (i1 + 2 > nq - 1)
        live = v1 == 1
        po = pend[other]
        pend_other = (
            jnp.where(live, jnp.where(sweep_done & jnp.logical_not(finished), i1 + 2, i1), po[0]),
            jnp.where(live, jnp.where(finished, 0, jnp.where(sweep_done, i1 + 2, j1 - 1)), po[1]),
            jnp.where(live, jnp.where(finished, 0, 1), po[2]))

        kb = k_ref[0, pl.ds(pl.multiple_of(j0 * blk, blk), blk), :]
        z_ref[wr] = _dot_nt(rows_of(q_ref, i0), kb)

        pend = (pend[0], pend_other) if lane == 0 else (pend_other, pend[1])
        carries = (carries[0], carry) if lane == 0 else (carry, carries[1])
        accs = (accs[0], acc) if lane == 0 else (acc, accs[1])
        return (pend, (i0, j0, v0), p1, p2, carries, accs)

    assert nq >= 2
    zero, one = jnp.int32(0), jnp.int32(1)
    idle = (zero, zero, zero)
    carry0 = jnp.zeros((HEADS_PER_STEP * blk, 1), F32)
    acc0 = jnp.zeros((blk, LANES), F32)
    init = (((zero, zero, one), (one, one, one)), idle, idle, idle, (carry0, carry0), (acc0, acc0))

    def in_flight(state):
        pend, p1, p2, p3, _, _ = state
        return (pend[0][2] + pend[1][2] + p1[2] + p2[2] + p3[2]) > 0

    lax.while_loop(in_flight, lambda st: step(step(st, 0), 1), init)


def _post_kernel(x1_ref, yc_ref, ysb_ref, p_ref, wo_ref, g2_ref, wg_ref, wu_ref, wd_ref,
                 gp_ref, wpg_ref, wpp_ref, o_ref, act_ref):
    cdim = yc_ref.shape[2]
    x2 = x1_ref[0] + _dot(yc_ref[0], wo_ref[:cdim, :]) + _dot(ysb_ref[0], wo_ref[cdim:, :])
    x3 = _swiglu_half_step(x2, g2_ref, wg_ref, wu_ref, wd_ref, act_ref)
    gate = jax.nn.sigmoid(_dot(_rmsnorm(x3, gp_ref[...]).astype(BF16), wpg_ref[...]))
    o_ref[0] = x3 + gate * _dot(p_ref[0].astype(BF16), wpp_ref[...])


def _resident(shape):
    return pl.BlockSpec(shape, lambda *_: (0,) * len(shape), pipeline_mode=pl.Buffered(1))


def _row_spec(width):
    return pl.BlockSpec((1, ROW_TILE, width), lambda b, s: (b, s, 0))


def _attention_constants():
    pos = jnp.arange(ATTN_BLOCK, dtype=jnp.int32)
    tri = (pos[:, None] > pos[None, :]).astype(BF16)
    causal = jnp.tile(pos[None, :] < pos[:, None], (HEADS_PER_STEP, 1))
    mask_bias = jnp.stack([jnp.full(causal.shape, MASK_BIG, F32),
                           jnp.where(causal, MASK_BIG, -MASK_BIG).astype(F32)])
    return tri, mask_bias


def _attention(q, k, v, tri, mask_bias):
    B, heads, S, _ = q.shape
    blk = ATTN_BLOCK
    rows = HEADS_PER_STEP * blk
    pair_spec = pl.BlockSpec((1, HEADS_PER_STEP, S, LANES), lambda b, hp: (b, hp, 0, 0))
    seq_spec = pl.BlockSpec((1, S, LANES), lambda b, hp: (b, 0, hp))
    return pl.pallas_call(
        _attn_kernel,
        grid=(B, heads // HEADS_PER_STEP),
        in_specs=[pair_spec, seq_spec, pair_spec, _resident(tri.shape), _resident(mask_bias.shape)],
        out_specs=seq_spec,
        out_shape=jax.ShapeDtypeStruct(k.shape, BF16),
        scratch_shapes=[pltpu.VMEM((2, rows, blk), F32), pltpu.VMEM((2, rows, blk), BF16),
                        pltpu.VMEM((2, rows, blk), F32), pltpu.VMEM((2, blk, rows), BF16)],
        compiler_params=pltpu.CompilerParams(
            dimension_semantics=("arbitrary", "arbitrary"), vmem_limit_bytes=VMEM_LIMIT_BYTES),
        name="attn",
    )(q, k, v, tri, mask_bias)


def _layer(x, p, w):
    B, S, D = x.shape
    d_ff = w["ffn1_wg"].shape[1]
    cdim = w["conv_w"].shape[1]
    sdim = w["bd"].shape[0]
    heads = sdim // HEAD_DIM
    assert S % ROW_TILE == 0 and S % ATTN_BLOCK == 0 and d_ff % FF_CHUNK == 0
    assert sdim % LANES == 0 and cdim == sdim
    grid = (B, S // ROW_TILE)
    dense_params = pltpu.CompilerParams(
        dimension_semantics=("arbitrary", "arbitrary"), vmem_limit_bytes=VMEM_LIMIT_BYTES)
    per_head_spec = pl.BlockSpec((1, heads, ROW_TILE, LANES), lambda b, s: (b, 0, s, 0))
    per_head_shape = jax.ShapeDtypeStruct((B, heads, S, LANES), BF16)

    pre_weights = [w["ffn1_norm"], w["ffn1_wg"], w["ffn1_wu"], w["ffn1_wd"], w["mix_norm"], w["w_in"],
                   w["conv_w"], w["conv_b"], w["q_gain"], w["k_gain"], w["bd"]]
    x1, yc, q, k, v = pl.pallas_call(
        _pre_kernel,
        grid=grid,
        in_specs=[_row_spec(D)] + [_resident(a.shape) for a in pre_weights],
        out_specs=[_row_spec(D), _row_spec(cdim), per_head_spec, _row_spec(sdim), per_head_spec],
        out_shape=[jax.ShapeDtypeStruct((B, S, D), F32),
                   jax.ShapeDtypeStruct((B, S, cdim), BF16),
                   per_head_shape,
                   jax.ShapeDtypeStruct((B, S, sdim), BF16),
                   per_head_shape],
        scratch_shapes=[pltpu.VMEM((ROW_TILE, d_ff), BF16), pltpu.VMEM((8, cdim), F32)],
        compiler_params=dense_params,
        name="pre",
    )(x, *pre_weights)

    ysb = _attention(q, k, v, w["tri"], w["mask_bias"])

    post_weights = [w["w_out"], w["ffn2_norm"], w["ffn2_wg"], w["ffn2_wu"], w["ffn2_wd"],
                    w["ple_norm"], w["ple_wg"], w["ple_wp"]]
    return pl.pallas_call(
        _post_kernel,
        grid=grid,
        in_specs=[_row_spec(D), _row_spec(cdim), _row_spec(sdim), _row_spec(p.shape[-1])]
        + [_resident(a.shape) for a in post_weights],
        out_specs=_row_spec(D),
        out_shape=jax.ShapeDtypeStruct((B, S, D), F32),
        scratch_shapes=[pltpu.VMEM((ROW_TILE, d_ff), BF16)],
        compiler_params=dense_params,
        name="post",
    )(x1, yc, ysb, p, *post_weights)


def kernel(x, p, ffn1_norm, ffn1_w_gate, ffn1_w_up, ffn1_w_down, mix_norm, w_in, conv_w, conv_b, q_norm, k_norm, w_out, ffn2_norm, ffn2_w_gate, ffn2_w_up, ffn2_w_down, ple_norm, ple_w_gate, ple_w_proj):
    depth = p.shape[0]
    sdim = w_out.shape[1] - conv_w.shape[2]
    heads = sdim // HEAD_DIM
    head_of = jnp.arange(sdim, dtype=jnp.int32) // HEAD_DIM
    bd = (head_of[:, None] == head_of[None, :]).astype(BF16)
    tri, mask_bias = _attention_constants()
    row = lambda a: a.reshape(1, -1).astype(F32)
    for i in range(depth):
        w = dict(
            ffn1_norm=row(ffn1_norm[i]), ffn1_wg=ffn1_w_gate[i].astype(BF16),
            ffn1_wu=ffn1_w_up[i].astype(BF16), ffn1_wd=ffn1_w_down[i].astype(BF16),
            mix_norm=row(mix_norm[i]), w_in=w_in[i].astype(BF16),
            conv_w=conv_w[i].astype(F32), conv_b=row(conv_b[i]),
            q_gain=row(jnp.tile(q_norm[i], heads)) * (HEAD_DIM ** -0.5 * LOG2_E),
            k_gain=row(jnp.tile(k_norm[i], heads)),
            bd=bd, tri=tri, mask_bias=mask_bias,
            w_out=w_out[i].astype(BF16), ffn2_norm=row(ffn2_norm[i]),
            ffn2_wg=ffn2_w_gate[i].astype(BF16), ffn2_wu=ffn2_w_up[i].astype(BF16),
            ffn2_wd=ffn2_w_down[i].astype(BF16), ple_norm=row(ple_norm[i]),
            ple_wg=ple_w_gate[i].astype(BF16), ple_wp=ple_w_proj[i].astype(BF16))
        x = _layer(x, p[i], w)
    return x
```
